```python
import math
import jax, jax.numpy as jnp
from jax import lax
import numpy as np

D_MODEL = 1024
BATCH = 16
SEQ = 2048
DEPTH = 2

CHUNK = 64
N_MEM = 256
EPS = 1e-6
MASK_VALUE = -1e30
D_MIX = D_MODEL
HG_HEADS = 4
HG_DIM = D_MIX // 4 // HG_HEADS
D_A = HG_HEADS * HG_DIM
D_B = D_MIX // 4
CONV_W = 3
CONV_GROUPS = 4
D_C = D_MIX - D_A - D_B
MLA_HEADS = 8
MLA_V = D_C // MLA_HEADS
MLA_NOPE = 64
MLA_ROPE = 32
Q_LORA = D_MODEL // 4
KV_LORA = D_MODEL // 8
ROPE_THETA = 10000.0
Q_BLOCK = 128
SPLIT_SIZES = (D_A, D_A, D_A, D_A, D_B, D_B, D_B, Q_LORA, KV_LORA, MLA_ROPE)
D_IN = 4 * D_A + 3 * D_B + Q_LORA + KV_LORA + MLA_ROPE
XA_HEADS = 4
XA_DIM = D_MODEL // XA_HEADS
N_GROUPS = 4
EXPERTS_PER_GROUP = 8
N_EXPERTS = N_GROUPS * EXPERTS_PER_GROUP
TOP_K = 2
D_EXPERT = D_MODEL // 4

kernel_name = 'hybrid_chunk_causal_block'


def rms_norm(x, g):
    xf = x.astype(jnp.float32)
    y = xf * lax.rsqrt(jnp.mean(xf * xf, axis=-1, keepdims=True) + EPS)
    return (y * g.astype(jnp.float32)).astype(x.dtype)


def group_rms_norm(x, g, groups):
    shp = x.shape
    xg = x.reshape(*shp[:-1], groups, shp[-1] // groups)
    return rms_norm(xg, g.reshape(groups, shp[-1] // groups)).reshape(shp)


def rope_tables(positions):
    half = MLA_ROPE // 2
    inv_freq = ROPE_THETA ** (-jnp.arange(half, dtype=jnp.float32) * 2.0 / MLA_ROPE)
    ang = positions.astype(jnp.float32)[..., None] * inv_freq
    return jnp.cos(ang), jnp.sin(ang)


def rotary(x, cos, sin):
    if x.ndim == 4:
        cos, sin = cos[:, :, None, :], sin[:, :, None, :]
    half = MLA_ROPE // 2
    xf = x.astype(jnp.float32)
    x1, x2 = xf[..., :half], xf[..., half:]
    return jnp.concatenate([x1 * cos - x2 * sin, x2 * cos + x1 * sin], axis=-1).astype(x.dtype)


def hgrn2_mixer(q, f_raw, i, g, lb, norm_g):
    B_, S_, _ = q.shape
    n = S_ // CHUNK
    lbf = lb.astype(jnp.float32)
    f = lbf + (1.0 - lbf) * jax.nn.sigmoid(f_raw.astype(jnp.float32))
    f = jnp.clip(f, 1e-6, 1.0)
    log_f = jnp.log(f)
    k = 1.0 - f
    q_act = jax.nn.silu(q.astype(jnp.float32))

    def heads(t):
        return t.astype(jnp.float32).reshape(B_, n, CHUNK, HG_HEADS, HG_DIM).transpose(1, 0, 3, 2, 4)

    qc, kc, vc = heads(q_act), heads(k), heads(i)
    bc = jnp.cumsum(heads(log_f), axis=3)
    causal = jnp.tril(jnp.ones((CHUNK, CHUNK), dtype=bool))[:, :, None]

    def step(state, xs):
        qb, kb, vb, bb = xs
        o_inter = jnp.einsum('bhld,bhdv->bhlv', qb * jnp.exp(bb), state)
        diff = bb[:, :, :, None, :] - bb[:, :, None, :, :]
        decay = jnp.where(causal, jnp.exp(jnp.minimum(diff, 0.0)), 0.0)
        scores = jnp.einsum('bhtd,bhsd,bhtsd->bhts', qb, kb, decay)
        o = o_inter + jnp.einsum('bhts,bhsv->bhtv', scores, vb)
        b_last = bb[:, :, -1:, :]
        state = jnp.exp(b_last[:, :, 0, :])[..., None] * state + jnp.einsum(
            'bhsd,bhsv->bhdv', kb * jnp.exp(b_last - bb), vb)
        return state, o

    s0 = jnp.zeros((B_, HG_HEADS, HG_DIM, HG_DIM), jnp.float32)
    _, o = lax.scan(step, s0, (qc, kc, vc, bc))
    o = o.transpose(1, 0, 3, 2, 4).reshape(B_, S_, HG_HEADS, HG_DIM)
    gate = jax.nn.silu(g.astype(jnp.float32)).reshape(B_, S_, HG_HEADS, HG_DIM)
    o = rms_norm(o, norm_g) * gate
    return o.reshape(B_, S_, D_A).astype(q.dtype)


def short_conv_mixer(b_gate, c_gate, h, w_conv, norm_g):
    u = c_gate * h
    y = lax.conv_general_dilated(u, w_conv[:, None, :].astype(u.dtype), window_strides=(1,),
                                 padding=[(CONV_W - 1, 0)], dimension_numbers=('NWC', 'WIO', 'NWC'),
                                 feature_group_count=D_B)
    return group_rms_norm(b_gate * y, norm_g, CONV_GROUPS)


def mla_mixer(c_q, c_kv, k_pe, cos, sin, q_norm, w_uq, kv_norm, w_ukv, out_norm):
    B_, S_, _ = c_q.shape
    q = (rms_norm(c_q, q_norm) @ w_uq).reshape(B_, S_, MLA_HEADS, MLA_NOPE + MLA_ROPE)
    q_nope, q_pe = q[..., :MLA_NOPE], rotary(q[..., MLA_NOPE:], cos, sin)
    kv = (rms_norm(c_kv, kv_norm) @ w_ukv).reshape(B_, S_, MLA_HEADS, MLA_NOPE + MLA_V)
    k_nope, v = kv[..., :MLA_NOPE], kv[..., MLA_NOPE:]
    k_rot = rotary(k_pe, cos, sin)
    scale = (MLA_NOPE + MLA_ROPE) ** -0.5
    key_chunk = jnp.arange(S_) // CHUNK
    nb = S_ // Q_BLOCK
    qn_b = q_nope.reshape(B_, nb, Q_BLOCK, MLA_HEADS, MLA_NOPE).transpose(1, 0, 2, 3, 4)
    qp_b = q_pe.reshape(B_, nb, Q_BLOCK, MLA_HEADS, MLA_ROPE).transpose(1, 0, 2, 3, 4)

    def block(args):
        qn, qp, blk = args
        s = (jnp.einsum('bqhd,bkhd->bhqk', qn, k_nope)
             + jnp.einsum('bqhr,bkr->bhqk', qp, k_rot)).astype(jnp.float32) * scale
        q_chunk = (blk * Q_BLOCK + jnp.arange(Q_BLOCK)) // CHUNK
        mask = key_chunk[None, :] <= q_chunk[:, None]
        s = jnp.where(mask, s, MASK_VALUE)
        p = jax.nn.softmax(s, axis=-1).astype(v.dtype)
        return jnp.einsum('bhqk,bkhv->bqhv', p, v)

    o = lax.map(block, (qn_b, qp_b, jnp.arange(nb)))
    o = o.transpose(1, 0, 2, 3, 4).reshape(B_, S_, D_C)
    return group_rms_norm(o, out_norm, MLA_HEADS)


def cross_attention(h, m, wq, wk, wv, wo):
    B_, S_, _ = h.shape
    q = (h @ wq).reshape(B_, S_, XA_HEADS, XA_DIM)
    k = (m @ wk).reshape(B_, m.shape[1], XA_HEADS, XA_DIM)
    v = (m @ wv).reshape(B_, m.shape[1], XA_HEADS, XA_DIM)
    s = jnp.einsum('bqhd,bkhd->bhqk', q, k).astype(jnp.float32) * (XA_DIM ** -0.5)
    p = jax.nn.softmax(s, axis=-1).astype(v.dtype)
    o = jnp.einsum('bhqk,bkhd->bqhd', p, v).reshape(B_, S_, D_MODEL)
    return o @ wo


def hier_moe(h, wg, bg, we, be, w_gate, w_up, w_down):
    B_, S_, D = h.shape
    t = h.reshape(B_ * S_, D)
    grp_prob = jax.nn.softmax((t @ wg).astype(jnp.float32) + bg.astype(jnp.float32), axis=-1)
    p_g, g_idx = lax.top_k(grp_prob, 1)
    exp_logits = jnp.einsum('td,gde->tge', t, we).astype(jnp.float32) + be.astype(jnp.float32)
    sel_logits = jnp.einsum('tg,tge->te', jax.nn.one_hot(g_idx[:, 0], N_GROUPS, dtype=jnp.float32), exp_logits)
    w_top, e_top = lax.top_k(jax.nn.softmax(sel_logits, axis=-1), TOP_K)
    w_top = w_top / jnp.sum(w_top, axis=-1, keepdims=True) * p_g
    expert_ids = g_idx * EXPERTS_PER_GROUP + e_top
    combine = jnp.einsum('tk,tke->te', w_top, jax.nn.one_hot(expert_ids, N_EXPERTS, dtype=jnp.float32))
    y = jnp.zeros((B_ * S_, D), jnp.float32)
    for e in range(N_EXPERTS):
        hid = jax.nn.silu(t @ w_gate[e]) * (t @ w_up[e])
        y = y + combine[:, e:e + 1] * (hid @ w_down[e]).astype(jnp.float32)
    return y.reshape(B_, S_, D).astype(h.dtype)


def setup_inputs(seed: int = 0) -> dict:
    key = jax.random.key(seed)
    ks = jax.random.split(key, 32)

    def nrm(k, shape, scale):
        return jax.random.normal(k, shape, jnp.float32) * scale

    def gain(k, shape):
        return 1.0 + 0.02 * jax.random.normal(k, shape, jnp.float32)

    L, D = DEPTH, D_MODEL
    positions = (jnp.arange(SEQ, dtype=jnp.int32)[None, :]
                 + jax.random.randint(ks[2], (BATCH, 1), 0, 64, dtype=jnp.int32) * CHUNK)
    return {
        'x': nrm(ks[0], (BATCH, SEQ, D), 1.0),
        'mem': nrm(ks[1], (BATCH, N_MEM, D), 1.0),
        'positions': positions,
        'norm_mix': gain(ks[3], (L, D)),
        'w_in': nrm(ks[4], (L, D, D_IN), D ** -0.5),
        'hgrn_lower_bounds': nrm(ks[5], (L, D_A), 1.0),
        'hgrn_norm': gain(ks[6], (L, HG_DIM)),
        'conv_w': nrm(ks[7], (L, CONV_W, D_B), CONV_W ** -0.5),
        'conv_norm': gain(ks[8], (L, D_B)),
        'mla_q_norm': gain(ks[9], (L, Q_LORA)),
        'mla_w_uq': nrm(ks[10], (L, Q_LORA, MLA_HEADS * (MLA_NOPE + MLA_ROPE)), Q_LORA ** -0.5),
        'mla_kv_norm': gain(ks[11], (L, KV_LORA)),
        'mla_w_ukv': nrm(ks[12], (L, KV_LORA, MLA_HEADS * (MLA_NOPE + MLA_V)), KV_LORA ** -0.5),
        'mla_out_norm': gain(ks[13], (L, D_C)),
        'w_out': nrm(ks[14], (L, D_MIX, D), D_MIX ** -0.5),
        'norm_cross': gain(ks[15], (L, D)),
        'norm_mem': gain(ks[16], (L, D)),
        'xa_wq': nrm(ks[17], (L, D, D), D ** -0.5),
        'xa_wk': nrm(ks[18], (L, D, D), D ** -0.5),
        'xa_wv': nrm(ks[19], (L, D, D), D ** -0.5),
        'xa_wo': nrm(ks[20], (L, D, D), D ** -0.5),
        'norm_ffn': gain(ks[21], (L, D)),
        'router_group_w': nrm(ks[22], (L, D, N_GROUPS), D ** -0.5),
        'router_group_b': nrm(ks[23], (L, N_GROUPS), 0.01),
        'router_expert_w': nrm(ks[24], (L, N_GROUPS, D, EXPERTS_PER_GROUP), D ** -0.5),
        'router_expert_b': nrm(ks[25], (L, N_GROUPS, EXPERTS_PER_GROUP), 0.01),
        'expert_w_gate': nrm(ks[26], (L, N_EXPERTS, D, D_EXPERT), D ** -0.5),
        'expert_w_up': nrm(ks[27], (L, N_EXPERTS, D, D_EXPERT), D ** -0.5),
        'expert_w_down': nrm(ks[28], (L, N_EXPERTS, D_EXPERT, D), D_EXPERT ** -0.5),
        'final_norm': gain(ks[29], (D,)),
    }


def reference(x, mem, positions, norm_mix, w_in, hgrn_lower_bounds, hgrn_norm, conv_w, conv_norm,
              mla_q_norm, mla_w_uq, mla_kv_norm, mla_w_ukv, mla_out_norm, w_out,
              norm_cross, norm_mem, xa_wq, xa_wk, xa_wv, xa_wo,
              norm_ffn, router_group_w, router_group_b, router_expert_w, router_expert_b,
              expert_w_gate, expert_w_up, expert_w_down, final_norm):
    lb_soft = jax.nn.softmax(hgrn_lower_bounds.astype(jnp.float32), axis=0)
    lbs = jnp.cumsum(lb_soft, axis=0) - lb_soft[0]
    cos, sin = rope_tables(positions)
    split_at = [int(s) for s in np.cumsum(SPLIT_SIZES)[:-1]]
    h = x
    for l in range(DEPTH):
        u = rms_norm(h, norm_mix[l])
        proj = u @ w_in[l]
        q_a, f_a, i_a, g_a, b_b, c_b, h_b, c_q, c_kv, k_pe = jnp.split(proj, split_at, axis=-1)
        o_a = hgrn2_mixer(q_a, f_a, i_a, g_a, lbs[l], hgrn_norm[l])
        o_b = short_conv_mixer(b_b, c_b, h_b, conv_w[l], conv_norm[l])
        o_c = mla_mixer(c_q, c_kv, k_pe, cos, sin, mla_q_norm[l], mla_w_uq[l],
                        mla_kv_norm[l], mla_w_ukv[l], mla_out_norm[l])
        mixed = jnp.concatenate([o_a.astype(u.dtype), o_b.astype(u.dtype), o_c.astype(u.dtype)], axis=-1)
        h = h + mixed @ w_out[l]
        h = h + cross_attention(rms_norm(h, norm_cross[l]), rms_norm(mem, norm_mem[l]),
                                xa_wq[l], xa_wk[l], xa_wv[l], xa_wo[l])
        h = h + hier_moe(rms_norm(h, norm_ffn[l]), router_group_w[l], router_group_b[l],
                         router_expert_w[l], router_expert_b[l],
                         expert_w_gate[l], expert_w_up[l], expert_w_down[l])
    return rms_norm(h, final_norm)
```

```python
import functools
import math

import jax
import jax.numpy as jnp
from jax import lax
from jax.experimental import pallas as pl
from jax.experimental.pallas import tpu as pltpu

F32 = jnp.float32
BF16 = jnp.bfloat16

D_MODEL = 1024
CHUNK = 64
EPS = 1e-6
MASK_VALUE = -1e30
HG_HEADS = 4
HG_DIM = 64
D_A = 256
D_B = 256
D_C = 512
MLA_HEADS = 8
MLA_NOPE = 64
MLA_ROPE = 32
MLA_V = 64
Q_LORA = 256
KV_LORA = 128
ROPE_THETA = 10000.0
XA_HEADS = 4
XA_DIM = 256
N_GROUPS = 4
EXPERTS_PER_GROUP = 8
N_EXPERTS = 32
D_EXPERT = 256
F_MIN = 1e-6

LANES = 128
D_IN_PAD = 2304
HEAD_PAD = 128
ROUTER_LANES = 128
EXPERT_LANE0 = N_GROUPS
EXPERT_TILE = 256
VMEM_LIMIT_BYTES = 52 * 1024 * 1024


def _params(semantics):
    return pltpu.CompilerParams(dimension_semantics=semantics,
                                vmem_limit_bytes=VMEM_LIMIT_BYTES)


def _dot(a, b):
    return jnp.dot(a, b, preferred_element_type=F32)


def _dot_nt(a, b):
    return lax.dot_general(a, b, (((1,), (1,)), ((), ())), preferred_element_type=F32)


def _dot_tn(a, b):
    return lax.dot_general(a, b, (((0,), (0,)), ((), ())), preferred_element_type=F32)


def _rms(x, g):
    ms = jnp.mean(x * x, axis=-1, keepdims=True)
    return x * lax.rsqrt(ms + EPS) * g


def _split2(x):
    hi = x.astype(BF16)
    lo = (x - hi.astype(F32)).astype(BF16)
    return hi, lo


def _split3(x):
    hi = x.astype(BF16)
    r = x - hi.astype(F32)
    mid = r.astype(BF16)
    lo = (r - mid.astype(F32)).astype(BF16)
    return hi, mid, lo


def _group_mean_sq(x, gmat):
    hi, lo = _split2(x * x)
    return _dot(hi, gmat) + _dot(lo, gmat)


def _silu(x):
    return x * jax.nn.sigmoid(x)


def _rope_kernel(pos_ref, invf_ref, c_ref, s1_ref, s2_ref):
    pos = pos_ref[0]
    seq = pos.shape[-1]
    ang = invf_ref[...] * pos
    cos = jnp.cos(ang)
    sin = jnp.sin(ang)
    z16 = jnp.zeros((16, seq), F32)
    z32 = jnp.zeros((32, seq), F32)
    z64 = jnp.zeros((64, seq), F32)
    one64 = jnp.ones((64, seq), F32)
    c_ref[0] = jnp.concatenate([one64, cos, cos, z32], axis=0).T
    s1_ref[0] = jnp.concatenate([z64, -sin, z16, z32], axis=0).T
    s2_ref[0] = jnp.concatenate([z64, z16, sin, z32], axis=0).T


def _rope_tables(positions):
    b, s = positions.shape
    half = MLA_ROPE // 2
    inv_freq = ROPE_THETA ** (-jnp.arange(half, dtype=F32) * 2.0 / MLA_ROPE)
    pos = positions.astype(F32).reshape(b, 1, s)
    tab = jax.ShapeDtypeStruct((b, s, LANES), F32)
    return pl.pallas_call(
        _rope_kernel,
        grid=(b,),
        in_specs=[pl.BlockSpec((1, 1, s), lambda i: (i, 0, 0)),
                  pl.BlockSpec((half, 1), lambda i: (0, 0))],
        out_specs=[pl.BlockSpec((1, s, LANES), lambda i: (i, 0, 0))] * 3,
        out_shape=[tab, tab, tab],
        compiler_params=_params(("arbitrary",)),
        name="rope_tables",
    )(pos, inv_freq.reshape(half, 1))


def _mem_kernel(mem_ref, g_ref, wk_ref, wv_ref, k_ref, v_ref):
    mn = _rms(mem_ref[0], g_ref[...]).astype(BF16)
    k_ref[0] = _dot(mn, wk_ref[...]).astype(BF16)
    v_ref[0] = _dot(mn, wv_ref[...]).astype(BF16)


def _mem_kv(mem, g, wk, wv):
    b, n, d = mem.shape
    out = jax.ShapeDtypeStruct((b, n, d), BF16)
    return pl.pallas_call(
        _mem_kernel,
        grid=(b,),
        in_specs=[pl.BlockSpec((1, n, d), lambda i: (i, 0, 0)),
                  pl.BlockSpec((1, d), lambda i: (0, 0)),
                  pl.BlockSpec((d, d), lambda i: (0, 0)),
                  pl.BlockSpec((d, d), lambda i: (0, 0))],
        out_specs=[pl.BlockSpec((1, n, d), lambda i: (i, 0, 0))] * 2,
        out_shape=[out, out],
        compiler_params=_params(("arbitrary",)),
        name="mem_kv",
    )(mem, g, wk, wv)


def _mix_kernel(h_ref, gmix_ref, win_ref, lb_ref, hgn_ref, cw_ref, cn_ref, qn_ref, wuq_ref,
                kvn_ref, wkk_ref, wkv_ref, c_ref, s1_ref, s2_ref, g256_ref,
                oab_ref, q_ref, k_ref, v_ref,
                proj_ref, state_ref, carry_ref):
    tm = h_ref.shape[1]
    nh = HG_HEADS

    @pl.when(pl.program_id(1) == 0)
    def _():
        state_ref[...] = jnp.zeros_like(state_ref)
        carry_ref[...] = jnp.zeros_like(carry_ref)

    u = _rms(h_ref[0], gmix_ref[...]).astype(BF16)
    proj_ref[...] = _dot(u, win_ref[...])

    row = lax.broadcasted_iota(jnp.int32, (CHUNK, CHUNK), 0)
    col = lax.broadcasted_iota(jnp.int32, (CHUNK, CHUNK), 1)
    tri = (col <= row).astype(BF16)
    t4 = lax.broadcasted_iota(jnp.int32, (nh * CHUNK, CHUNK), 0) & (CHUNK - 1)
    s4 = lax.broadcasted_iota(jnp.int32, (nh * CHUNK, CHUNK), 1)

    def level_mask(m):
        sh = int(math.log2(m))
        same_pair = (t4 >> (sh + 1)) == (s4 >> (sh + 1))
        return same_pair & (((t4 >> sh) & 1) == 1) & (((s4 >> sh) & 1) == 0)

    diag_mask = ((t4 >> 3) == (s4 >> 3)) & (s4 <= t4)
    lane_head = lax.broadcasted_iota(jnp.int32, (CHUNK, D_A), 1) >> 6
    bd_mask = (lax.broadcasted_iota(jnp.int32, (D_A, D_A), 0) >> 6) == \
              (lax.broadcasted_iota(jnp.int32, (D_A, D_A), 1) >> 6)
    lb = lb_ref[...]
    hgn = hgn_ref[...]
    g256 = g256_ref[...]

    def chunk_body(c, carry):
        r0 = pl.multiple_of(c * CHUNK, CHUNK)
        rows = pl.ds(r0, CHUNK)
        q = proj_ref[rows, 0:256]
        fr = proj_ref[rows, 256:512]
        iv = proj_ref[rows, 512:768]
        gg = proj_ref[rows, 768:1024]
        f = lb + (1.0 - lb) * jax.nn.sigmoid(fr)
        f = jnp.clip(f, F_MIN, 1.0)
        logf = jnp.log(f)
        kk = 1.0 - f
        qa = _silu(q)
        hi, mid, lo = _split3(logf)
        b = _dot(tri, hi) + _dot(tri, mid) + _dot(tri, lo)
        blast = b[CHUNK - 1:CHUNK, :]

        levels = []
        for m in (32, 16, 8):
            eq, ek = [], []
            for j in range(CHUNK // m):
                blk = b[j * m:(j + 1) * m]
                eq.append(blk - b[j * m - 1:j * m] if j > 0 else blk)
                ek.append(b[(j + 1) * m - 1:(j + 1) * m] - blk)
            levels.append((jnp.concatenate(eq, axis=0), jnp.concatenate(ek, axis=0), level_mask(m)))
        ed = jnp.concatenate([b[8 * j:8 * j + 8] - b[8 * j + 3:8 * j + 4] for j in range(CHUNK // 8)],
                             axis=0)
        levels.append((ed, -ed, diag_mask))

        vb = iv.astype(BF16)
        s_all = jnp.zeros((nh * CHUNK, CHUNK), F32)
        for eq, ek, msk in levels:
            qm = qa * jnp.exp(eq)
            km = (kk * jnp.exp(ek)).astype(BF16)
            qst = jnp.concatenate([jnp.where(lane_head == hd, qm, 0.0) for hd in range(nh)],
                                  axis=0).astype(BF16)
            s_all = s_all + jnp.where(msk, _dot_nt(qst, km), 0.0)
        pv = _dot(s_all.astype(BF16), vb)
        o = jnp.zeros((CHUNK, D_A), F32)
        for hd in range(nh):
            o = o + jnp.where(lane_head == hd, pv[hd * CHUNK:(hd + 1) * CHUNK], 0.0)

        st = state_ref[...]
        qdec = (qa * jnp.exp(b)).astype(BF16)
        o = o + _dot_nt(qdec, st.astype(BF16))
        kdec = (kk * jnp.exp(blast - b)).astype(BF16)
        upd = _dot_tn(vb, kdec)
        state_ref[...] = st * jnp.exp(blast) + jnp.where(bd_mask, upd, 0.0)

        ms = _group_mean_sq(o, g256)
        oa = o * lax.rsqrt(ms + EPS) * hgn * _silu(gg)
        oab_ref[0, rows, 0:D_A] = oa.astype(BF16)
        return carry

    lax.fori_loop(0, tm // CHUNK, chunk_body, 0)

    bb = proj_ref[:, 1024:1280]
    uu = proj_ref[:, 1280:1536] * proj_ref[:, 1536:1792]
    prev = carry_ref[...]
    rid = lax.broadcasted_iota(jnp.int32, (tm, D_B), 0)
    u1 = jnp.where(rid == 0, prev[7:8, :], pltpu.roll(uu, 1, 0))
    u2 = jnp.where(rid == 0, prev[6:7, :], jnp.where(rid == 1, prev[7:8, :], pltpu.roll(uu, 2, 0)))
    cw = cw_ref[...]
    y = cw[0:1, :] * u2 + cw[1:2, :] * u1 + cw[2:3, :] * uu
    carry_ref[...] = uu[tm - 8:tm, :]
    z = bb * y
    ob = z * lax.rsqrt(_group_mean_sq(z, g256) + EPS) * cn_ref[...]
    oab_ref[0, :, D_A:D_A + D_B] = ob.astype(BF16)

    cmat = c_ref[0]
    s1 = s1_ref[0]
    s2 = s2_ref[0]
    cq = _rms(proj_ref[:, 1792:2048], qn_ref[...]).astype(BF16)
    qf = _dot(cq, wuq_ref[...])
    cfull = jnp.concatenate([cmat] * MLA_HEADS, axis=1)
    s1full = jnp.concatenate([s1] * MLA_HEADS, axis=1)
    s2full = jnp.concatenate([s2] * MLA_HEADS, axis=1)
    width = MLA_HEADS * HEAD_PAD
    half = MLA_ROPE // 2
    qrot = qf * cfull + pltpu.roll(qf, width - half, 1) * s1full + pltpu.roll(qf, half, 1) * s2full
    scale = (MLA_NOPE + MLA_ROPE) ** -0.5
    q_ref[0] = (qrot * scale).astype(BF16)

    ckv = _rms(proj_ref[:, 2048:2176], kvn_ref[...]).astype(BF16)
    kf = _dot(ckv, wkk_ref[...])
    v_ref[0] = _dot(ckv, wkv_ref[...]).astype(BF16)
    kpe = proj_ref[:, 2176:2304]
    krot = kpe * cmat + pltpu.roll(kpe, HEAD_PAD - half, 1) * s1 + pltpu.roll(kpe, half, 1) * s2
    k_ref[0] = (kf + jnp.concatenate([krot] * MLA_HEADS, axis=1)).astype(BF16)


def _mixer(h, lw, tabs, g256, tm):
    b, s, d = h.shape
    cmat, s1, s2 = tabs
    grid = (b, s // tm)
    const = lambda i, j: (0, 0)
    tile = lambda i, j: (i, j, 0)
    width = MLA_HEADS * HEAD_PAD
    in_specs = [
        pl.BlockSpec((1, tm, d), tile),
        pl.BlockSpec((1, d), const),
        pl.BlockSpec((d, D_IN_PAD), const),
        pl.BlockSpec((1, D_A), const),
        pl.BlockSpec((1, D_A), const),
        pl.BlockSpec((3, D_B), const),
        pl.BlockSpec((1, D_B), const),
        pl.BlockSpec((1, Q_LORA), const),
        pl.BlockSpec((Q_LORA, width), const),
        pl.BlockSpec((1, KV_LORA), const),
        pl.BlockSpec((KV_LORA, width), const),
        pl.BlockSpec((KV_LORA, D_C), const),
        pl.BlockSpec((1, tm, LANES), tile),
        pl.BlockSpec((1, tm, LANES), tile),
        pl.BlockSpec((1, tm, LANES), tile),
        pl.BlockSpec((D_A, D_A), const),
    ]
    out_specs = [
        pl.BlockSpec((1, tm, D_A + D_B), tile),
        pl.BlockSpec((1, tm, width), tile),
        pl.BlockSpec((1, tm, width), tile),
        pl.BlockSpec((1, tm, D_C), tile),
    ]
    out_shape = [
        jax.ShapeDtypeStruct((b, s, D_A + D_B), BF16),
        jax.ShapeDtypeStruct((b, s, width), BF16),
        jax.ShapeDtypeStruct((b, s, width), BF16),
        jax.ShapeDtypeStruct((b, s, D_C), BF16),
    ]
    return pl.pallas_call(
        _mix_kernel,
        grid=grid,
        in_specs=in_specs,
        out_specs=out_specs,
        out_shape=out_shape,
        scratch_shapes=[pltpu.VMEM((tm, D_IN_PAD), F32),
                        pltpu.VMEM((D_A, D_A), F32),
                        pltpu.VMEM((8, D_B), F32)],
        compiler_params=_params(("arbitrary", "arbitrary")),
        name="mixer",
    )(h, lw["norm_mix"], lw["w_in"], lw["lb"], lw["hgrn_norm"], lw["conv_w"], lw["conv_norm"],
      lw["q_norm"], lw["w_uq"], lw["kv_norm"], lw["w_kk"], lw["w_kv"], cmat, s1, s2, g256)


def _mla_kernel(q_ref, k_ref, v_ref, on_ref, g256_ref, o_ref):
    tq = q_ref.shape[1]
    qi = pl.program_id(1)
    qc = lax.broadcasted_iota(jnp.int32, (tq, tq), 0) >> 6
    kc = lax.broadcasted_iota(jnp.int32, (tq, tq), 1) >> 6
    dmask = kc <= qc
    lane = lax.broadcasted_iota(jnp.int32, (tq, LANES), 1)
    diag0 = pl.multiple_of(qi * tq, tq)

    def step(qh, kh, vh, m, l, acc, mask):
        s = _dot_nt(qh, kh)
        if mask is not None:
            s = jnp.where(mask, s, MASK_VALUE)
        mn = jnp.maximum(m, jnp.max(s, axis=-1, keepdims=True))
        a = jnp.exp(m - mn)
        p = jnp.exp(s - mn)
        l = a * l + jnp.sum(p, axis=-1, keepdims=True)
        acc = a * acc + _dot(p.astype(BF16), vh)
        return mn, l, acc

    pairs = []
    for pr in range(MLA_HEADS // 2):
        vcols = slice(pr * LANES, (pr + 1) * LANES)
        res = []
        for hh in range(2):
            hd = 2 * pr + hh
            hcols = slice(hd * HEAD_PAD, (hd + 1) * HEAD_PAD)
            qh = q_ref[0, :, hcols]

            def body(j, carry, hcols=hcols, vcols=vcols, qh=qh):
                rows = pl.ds(pl.multiple_of(j * tq, tq), tq)
                return step(qh, k_ref[0, rows, hcols], v_ref[0, rows, vcols], *carry, None)

            init = (jnp.full((tq, 1), MASK_VALUE, F32), jnp.zeros((tq, 1), F32),
                    jnp.zeros((tq, LANES), F32))
            m, l, acc = lax.fori_loop(0, qi, body, init)
            rows = pl.ds(diag0, tq)
            m, l, acc = step(qh, k_ref[0, rows, hcols], v_ref[0, rows, vcols], m, l, acc, dmask)
            res.append(acc / l)
        pairs.append(jnp.where(lane < MLA_V, res[0], res[1]))
    g256 = g256_ref[...]
    on = on_ref[...]
    for half in range(2):
        o = jnp.concatenate(pairs[2 * half:2 * half + 2], axis=1)
        ms = _group_mean_sq(o, g256)
        cols = slice(half * 256, (half + 1) * 256)
        o_ref[0, :, cols] = (o * lax.rsqrt(ms + EPS) * on[:, cols]).astype(BF16)


def _mla_attention(q, k, v, out_norm, g256, tq):
    b, s, width = q.shape
    return pl.pallas_call(
        _mla_kernel,
        grid=(b, s // tq),
        in_specs=[pl.BlockSpec((1, tq, width), lambda i, j: (i, j, 0)),
                  pl.BlockSpec((1, s, width), lambda i, j: (i, 0, 0)),
                  pl.BlockSpec((1, s, D_C), lambda i, j: (i, 0, 0)),
                  pl.BlockSpec((1, D_C), lambda i, j: (0, 0)),
                  pl.BlockSpec((D_A, D_A), lambda i, j: (0, 0))],
        out_specs=pl.BlockSpec((1, tq, D_C), lambda i, j: (i, j, 0)),
        out_shape=jax.ShapeDtypeStruct((b, s, D_C), BF16),
        compiler_params=_params(("arbitrary", "arbitrary")),
        name="mla_attention",
    )(q, k, v, out_norm, g256)


def _post_kernel(h_ref, oab_ref, oc_ref, wout_ref, gx_ref, wq_ref, kx_ref, vx_ref, wo_ref,
                 gf_ref, wrh_ref, wrl_ref, rb_ref,
                 h2_ref, t_ref, ri_ref, cnt_ref, cnt_scr):
    tm = h_ref.shape[1]

    @pl.when((pl.program_id(0) == 0) & (pl.program_id(1) == 0))
    def _():
        cnt_scr[...] = jnp.zeros_like(cnt_scr)

    mixed = jnp.concatenate([oab_ref[0], oc_ref[0]], axis=1)
    h1 = h_ref[0] + _dot(mixed, wout_ref[...])

    hn = _rms(h1, gx_ref[...]).astype(BF16)
    qx = (_dot(hn, wq_ref[...]) * (XA_DIM ** -0.5)).astype(BF16)
    heads = []
    for hd in range(XA_HEADS):
        cols = slice(hd * XA_DIM, (hd + 1) * XA_DIM)
        s = _dot_nt(qx[:, cols], kx_ref[0, :, cols])
        e = jnp.exp(s - jnp.max(s, axis=-1, keepdims=True))
        p = e / jnp.sum(e, axis=-1, keepdims=True)
        heads.append(_dot(p.astype(BF16), vx_ref[0, :, cols]))
    ox = jnp.concatenate(heads, axis=1).astype(BF16)
    h2 = h1 + _dot(ox, wo_ref[...])
    h2_ref[0] = h2

    t = _rms(h2, gf_ref[...])
    t_ref[0] = t

    thi, tlo = _split2(t)
    wrh = wrh_ref[...]
    logits = _dot(thi, wrh) + _dot(tlo, wrh) + _dot(thi, wrl_ref[...]) + rb_ref[...]
    lane = lax.broadcasted_iota(jnp.int32, (tm, ROUTER_LANES), 1)
    neg = jnp.float32(-jnp.inf)
    big = jnp.int32(ROUTER_LANES)

    glog = jnp.where(lane < N_GROUPS, logits, neg)
    gmax = jnp.max(glog, axis=-1, keepdims=True)
    p_g = 1.0 / jnp.sum(jnp.exp(glog - gmax), axis=-1, keepdims=True)
    g_idx = jnp.min(jnp.where(glog == gmax, lane, big), axis=-1, keepdims=True)

    lo_lane = EXPERT_LANE0 + g_idx * EXPERTS_PER_GROUP
    sel = jnp.where((lane >= lo_lane) & (lane < lo_lane + EXPERTS_PER_GROUP), logits, neg)
    m1 = jnp.max(sel, axis=-1, keepdims=True)
    i1 = jnp.min(jnp.where(sel == m1, lane, big), axis=-1, keepdims=True)
    sel2 = jnp.where(lane == i1, neg, sel)
    m2 = jnp.max(sel2, axis=-1, keepdims=True)
    i2 = jnp.min(jnp.where(sel2 == m2, lane, big), axis=-1, keepdims=True)
    e2 = jnp.exp(m2 - m1)
    w1 = p_g / (1.0 + e2)
    w2 = p_g * e2 / (1.0 + e2)

    hit1 = lane == i1
    hit2 = lane == i2
    onehot = jnp.where(hit1 | hit2, 1.0, 0.0)
    r = lax.broadcasted_iota(jnp.int32, (tm, tm), 0)
    c = lax.broadcasted_iota(jnp.int32, (tm, tm), 1)
    before = _dot((c < r).astype(BF16), onehot.astype(BF16)) + cnt_scr[...]
    rank1 = jnp.sum(jnp.where(hit1, before, 0.0), axis=-1, keepdims=True)
    rank2 = jnp.sum(jnp.where(hit2, before, 0.0), axis=-1, keepdims=True)
    cnt_scr[...] = cnt_scr[...] + jnp.sum(onehot, axis=0, keepdims=True)
    cnt_ref[...] = cnt_scr[...]

    e1f = (i1 - EXPERT_LANE0).astype(F32)
    e2f = (i2 - EXPERT_LANE0).astype(F32)
    info = jnp.where(lane == 0, e1f,
           jnp.where(lane == 1, e2f,
           jnp.where(lane == 2, w1,
           jnp.where(lane == 3, w2,
           jnp.where(lane == 4, rank1,
           jnp.where(lane == 5, rank2, 0.0))))))
    ri_ref[0] = info


def _post(h, oab, oc, kx, vx, lw, tm):
    b, s, d = h.shape
    n_mem = kx.shape[1]
    const = lambda i, j: (0, 0)
    tile = lambda i, j: (i, j, 0)
    in_specs = [
        pl.BlockSpec((1, tm, d), tile),
        pl.BlockSpec((1, tm, D_A + D_B), tile),
        pl.BlockSpec((1, tm, D_C), tile),
        pl.BlockSpec((d, d), const),
        pl.BlockSpec((1, d), const),
        pl.BlockSpec((d, d), const),
        pl.BlockSpec((1, n_mem, d), lambda i, j: (i, 0, 0)),
        pl.BlockSpec((1, n_mem, d), lambda i, j: (i, 0, 0)),
        pl.BlockSpec((d, d), const),
        pl.BlockSpec((1, d), const),
        pl.BlockSpec((d, ROUTER_LANES), const),
        pl.BlockSpec((d, ROUTER_LANES), const),
        pl.BlockSpec((1, ROUTER_LANES), const),
    ]
    out_specs = [
        pl.BlockSpec((1, tm, d), tile),
        pl.BlockSpec((1, tm, d), tile),
        pl.BlockSpec((1, tm, ROUTER_LANES), tile),
        pl.BlockSpec((1, ROUTER_LANES), const),
    ]
    out_shape = [
        jax.ShapeDtypeStruct((b, s, d), F32),
        jax.ShapeDtypeStruct((b, s, d), F32),
        jax.ShapeDtypeStruct((b, s, ROUTER_LANES), F32),
        jax.ShapeDtypeStruct((1, ROUTER_LANES), F32),
    ]
    return pl.pallas_call(
        _post_kernel,
        grid=(b, s // tm),
        in_specs=in_specs,
        out_specs=out_specs,
        out_shape=out_shape,
        scratch_shapes=[pltpu.VMEM((1, ROUTER_LANES), F32)],
        compiler_params=_params(("arbitrary", "arbitrary")),
        name="post",
    )(h, oab, oc, lw["w_out"], lw["norm_cross"], lw["xa_wq"], kx, vx, lw["xa_wo"],
      lw["norm_ffn"], lw["wr_hi"], lw["wr_lo"], lw["router_b"])


def _row_copy(src, dst, sem):
    return pltpu.make_async_copy(src, dst, sem)


def _dispatch_kernel(tstart_ref, e_ref, r_ref, t_ref, xs_in_ref, xs_ref, sem):
    del xs_in_ref
    tm = t_ref.shape[0]

    def issue(i, carry):
        for k in range(2):
            slot = tstart_ref[e_ref[2 * i + k]] * EXPERT_TILE + r_ref[2 * i + k]
            _row_copy(t_ref.at[pl.ds(i, 1)], xs_ref.at[pl.ds(slot, 1)], sem).start()
        return carry

    lax.fori_loop(0, tm, issue, 0)

    def drain(i, carry):
        for k in range(2):
            _row_copy(t_ref.at[pl.ds(0, 1)], xs_ref.at[pl.ds(0, 1)], sem).wait()
        return carry

    lax.fori_loop(0, tm, drain, 0)


def _dispatch(t2d, tstart, eidx, rank, n_slots, tm):
    n_tok, d = t2d.shape
    xs0 = jnp.zeros((n_slots, d), F32)
    return pl.pallas_call(
        _dispatch_kernel,
        grid=(n_tok // tm,),
        in_specs=[pl.BlockSpec(memory_space=pltpu.SMEM),
                  pl.BlockSpec((2 * tm,), lambda i: (i,), memory_space=pltpu.SMEM),
                  pl.BlockSpec((2 * tm,), lambda i: (i,), memory_space=pltpu.SMEM),
                  pl.BlockSpec((tm, d), lambda i: (i, 0)),
                  pl.BlockSpec(memory_space=pl.ANY)],
        out_specs=pl.BlockSpec(memory_space=pl.ANY),
        out_shape=jax.ShapeDtypeStruct((n_slots, d), F32),
        scratch_shapes=[pltpu.SemaphoreType.DMA],
        input_output_aliases={4: 0},
        compiler_params=_params(("arbitrary",)),
        name="moe_dispatch",
    )(tstart, eidx, rank, t2d, xs0)


def _expert_kernel(te_ref, nu_ref, x_ref, wgu_ref, wd_ref, y_ref):
    del te_ref
    i = pl.program_id(0)

    @pl.when(i < nu_ref[0])
    def _():
        xb = x_ref[...].astype(BF16)
        gu = _dot(xb, wgu_ref[0])
        hid = (_silu(gu[:, :D_EXPERT]) * gu[:, D_EXPERT:]).astype(BF16)
        y_ref[...] = _dot(hid, wd_ref[0])

    @pl.when(i >= nu_ref[0])
    def _():
        y_ref[...] = jnp.zeros_like(y_ref)


def _experts(xs, tile_expert, n_used, wgu, wd):
    n_slots, d = xs.shape
    n_tiles = n_slots // EXPERT_TILE
    grid_spec = pltpu.PrefetchScalarGridSpec(
        num_scalar_prefetch=2,
        grid=(n_tiles,),
        in_specs=[pl.BlockSpec((EXPERT_TILE, d), lambda i, te, nu: (i, 0)),
                  pl.BlockSpec((1, d, 2 * D_EXPERT), lambda i, te, nu: (te[i], 0, 0)),
                  pl.BlockSpec((1, D_EXPERT, d), lambda i, te, nu: (te[i], 0, 0))],
        out_specs=pl.BlockSpec((EXPERT_TILE, d), lambda i, te, nu: (i, 0)),
    )
    return pl.pallas_call(
        _expert_kernel,
        grid_spec=grid_spec,
        out_shape=jax.ShapeDtypeStruct((n_slots, d), F32),
        compiler_params=_params(("arbitrary",)),
        name="moe_experts",
    )(tile_expert, n_used, xs, wgu, wd)


def _combine_kernel(tstart_ref, e_ref, r_ref, h2_ref, ri_ref, gfin_ref, ys_ref, out_ref, buf, sem,
                    *, final_norm):
    tm = h2_ref.shape[0]

    def issue(i, carry):
        for k in range(2):
            slot = tstart_ref[e_ref[2 * i + k]] * EXPERT_TILE + r_ref[2 * i + k]
            _row_copy(ys_ref.at[pl.ds(slot, 1)], buf.at[k, pl.ds(i, 1)], sem).start()
        return carry

    lax.fori_loop(0, tm, issue, 0)

    def drain(i, carry):
        for k in range(2):
            _row_copy(ys_ref.at[pl.ds(0, 1)], buf.at[k, pl.ds(0, 1)], sem).wait()
        return carry

    lax.fori_loop(0, tm, drain, 0)

    ri = ri_ref[...]
    h3 = h2_ref[...] + ri[:, 2:3] * buf[0] + ri[:, 3:4] * buf[1]
    if final_norm:
        h3 = _rms(h3, gfin_ref[...])
    out_ref[...] = h3


def _combine(h2, rinfo, ys, tstart, eidx, rank, gfin, tm, final_norm):
    n_tok, d = h2.shape
    return pl.pallas_call(
        functools.partial(_combine_kernel, final_norm=final_norm),
        grid=(n_tok // tm,),
        in_specs=[pl.BlockSpec(memory_space=pltpu.SMEM),
                  pl.BlockSpec((2 * tm,), lambda i: (i,), memory_space=pltpu.SMEM),
                  pl.BlockSpec((2 * tm,), lambda i: (i,), memory_space=pltpu.SMEM),
                  pl.BlockSpec((tm, d), lambda i: (i, 0)),
                  pl.BlockSpec((tm, ROUTER_LANES), lambda i: (i, 0)),
                  pl.BlockSpec((1, d), lambda i: (0, 0)),
                  pl.BlockSpec(memory_space=pl.ANY)],
        out_specs=pl.BlockSpec((tm, d), lambda i: (i, 0)),
        out_shape=jax.ShapeDtypeStruct((n_tok, d), F32),
        scratch_shapes=[pltpu.VMEM((2, tm, d), F32), pltpu.SemaphoreType.DMA],
        compiler_params=_params(("arbitrary",)),
        name="moe_combine",
    )(tstart, eidx, rank, h2, rinfo, gfin, ys)


def _moe(h2, t, rinfo, counts, lw, gfin, final_norm):
    b, s, d = h2.shape
    n_tok = b * s
    tm = min(512, n_tok)
    n_tiles = (2 * n_tok) // EXPERT_TILE + N_EXPERTS
    ri2 = rinfo.reshape(n_tok, ROUTER_LANES)
    cnt = counts[0, EXPERT_LANE0:EXPERT_LANE0 + N_EXPERTS].astype(jnp.int32)
    ntile = (cnt + EXPERT_TILE - 1) // EXPERT_TILE
    tend = jnp.cumsum(ntile)
    tstart = (tend - ntile).astype(jnp.int32)
    tile_expert = jnp.minimum(
        jnp.searchsorted(tend, jnp.arange(n_tiles, dtype=jnp.int32), side="right"),
        N_EXPERTS - 1).astype(jnp.int32)
    n_used = tend[-1:].astype(jnp.int32)
    eidx = ri2[:, 0:2].astype(jnp.int32).reshape(-1)
    rank = ri2[:, 4:6].astype(jnp.int32).reshape(-1)

    xs = _dispatch(t.reshape(n_tok, d), tstart, eidx, rank, n_tiles * EXPERT_TILE, tm)
    ys = _experts(xs, tile_expert, n_used, lw["w_gu"], lw["w_down"])
    out = _combine(h2.reshape(n_tok, d), ri2, ys, tstart, eidx, rank, gfin, tm, final_norm)
    return out.reshape(b, s, d)


def _layer_weights(l, lbs, norm_mix, w_in, hgrn_norm, conv_w, conv_norm, mla_q_norm, mla_w_uq,
                   mla_kv_norm, mla_w_ukv, mla_out_norm, w_out, norm_cross, norm_mem, xa_wq, xa_wk,
                   xa_wv, xa_wo, norm_ffn, router_group_w, router_group_b, router_expert_w,
                   router_expert_b, expert_w_gate, expert_w_up, expert_w_down):
    d = D_MODEL
    row = lambda x: x.reshape(1, -1).astype(F32)
    d_kpe = 4 * D_A + 3 * D_B + Q_LORA + KV_LORA
    win = w_in[l]
    win_p = jnp.concatenate([win[:, :d_kpe], jnp.zeros((d, MLA_NOPE), F32), win[:, d_kpe:],
                             jnp.zeros((d, HEAD_PAD - MLA_NOPE - MLA_ROPE), F32)], axis=1)
    wuq = mla_w_uq[l].reshape(Q_LORA, MLA_HEADS, MLA_NOPE + MLA_ROPE)
    wuq_p = jnp.pad(wuq, ((0, 0), (0, 0), (0, HEAD_PAD - MLA_NOPE - MLA_ROPE)))
    wukv = mla_w_ukv[l].reshape(KV_LORA, MLA_HEADS, MLA_NOPE + MLA_V)
    wkk = jnp.pad(wukv[:, :, :MLA_NOPE], ((0, 0), (0, 0), (0, HEAD_PAD - MLA_NOPE)))
    wkv = wukv[:, :, MLA_NOPE:]
    wr = jnp.zeros((d, ROUTER_LANES), F32)
    wr = wr.at[:, :N_GROUPS].set(router_group_w[l])
    wr = wr.at[:, EXPERT_LANE0:EXPERT_LANE0 + N_EXPERTS].set(
        jnp.transpose(router_expert_w[l], (1, 0, 2)).reshape(d, N_EXPERTS))
    wr_hi = wr.astype(BF16)
    rb = jnp.zeros((1, ROUTER_LANES), F32)
    rb = rb.at[0, :N_GROUPS].set(router_group_b[l])
    rb = rb.at[0, EXPERT_LANE0:EXPERT_LANE0 + N_EXPERTS].set(router_expert_b[l].reshape(-1))
    return {
        "norm_mix": row(norm_mix[l]),
        "w_in": win_p.astype(BF16),
        "lb": row(lbs[l]),
        "hgrn_norm": row(jnp.tile(hgrn_norm[l], HG_HEADS)),
        "conv_w": conv_w[l].astype(F32),
        "conv_norm": row(conv_norm[l]),
        "q_norm": row(mla_q_norm[l]),
        "w_uq": wuq_p.reshape(Q_LORA, MLA_HEADS * HEAD_PAD).astype(BF16),
        "kv_norm": row(mla_kv_norm[l]),
        "w_kk": wkk.reshape(KV_LORA, MLA_HEADS * HEAD_PAD).astype(BF16),
        "w_kv": wkv.reshape(KV_LORA, D_C).astype(BF16),
        "out_norm": row(mla_out_norm[l]),
        "w_out": w_out[l].astype(BF16),
        "norm_cross": row(norm_cross[l]),
        "norm_mem": row(norm_mem[l]),
        "xa_wq": xa_wq[l].astype(BF16),
        "xa_wk": xa_wk[l].astype(BF16),
        "xa_wv": xa_wv[l].astype(BF16),
        "xa_wo": xa_wo[l].astype(BF16),
        "norm_ffn": row(norm_ffn[l]),
        "wr_hi": wr_hi,
        "wr_lo": (wr - wr_hi.astype(F32)).astype(BF16),
        "router_b": rb,
        "w_gu": jnp.concatenate([expert_w_gate[l], expert_w_up[l]], axis=-1).astype(BF16),
        "w_down": expert_w_down[l].astype(BF16),
    }


def kernel(x, mem, positions, norm_mix, w_in, hgrn_lower_bounds, hgrn_norm, conv_w, conv_norm, mla_q_norm, mla_w_uq, mla_kv_norm, mla_w_ukv, mla_out_norm, w_out, norm_cross, norm_mem, xa_wq, xa_wk, xa_wv, xa_wo, norm_ffn, router_group_w, router_group_b, router_expert_w, router_expert_b, expert_w_gate, expert_w_up, expert_w_down, final_norm):
    b, s, d = x.shape
    depth = norm_mix.shape[0]
    assert d == D_MODEL and s % CHUNK == 0
    tm = min(512, s)
    tq = min(256, s)

    lb_soft = jax.nn.softmax(hgrn_lower_bounds.astype(F32), axis=0)
    lbs = jnp.cumsum(lb_soft, axis=0) - lb_soft[0]
    lane = jnp.arange(D_A)
    g256 = jnp.where((lane[:, None] // HG_DIM) == (lane[None, :] // HG_DIM), 1.0 / HG_DIM, 0.0).astype(BF16)
    gfin = final_norm.reshape(1, d).astype(F32)

    tabs = _rope_tables(positions)
    h = x
    for l in range(depth):
        lw = _layer_weights(l, lbs, norm_mix, w_in, hgrn_norm, conv_w, conv_norm, mla_q_norm,
                            mla_w_uq, mla_kv_norm, mla_w_ukv, mla_out_norm, w_out, norm_cross,
                            norm_mem, xa_wq, xa_wk, xa_wv, xa_wo, norm_ffn, router_group_w,
                            router_group_b, router_expert_w, router_expert_b, expert_w_gate,
                            expert_w_up, expert_w_down)
        kx, vx = _mem_kv(mem, lw["norm_mem"], lw["xa_wk"], lw["xa_wv"])
        oab, q, k, v = _mixer(h, lw, tabs, g256, tm)
        oc = _mla_attention(q, k, v, lw["out_norm"], g256, tq)
        h2, t, rinfo, counts = _post(h, oab, oc, kx, vx, lw, tm)
        h = _moe(h2, t, rinfo, counts, lw, gfin, final_norm=(l == depth - 1))
    return h
```
